```python
import jax
import jax.numpy as jnp
from jax import lax
import numpy as np

D_MODEL = 1024
BATCH = 2
SEQ = 8192
DEPTH = 2

GRID_W = 64
CTX_LEN = 256
N_MIXERS = 4
GROUP_W = D_MODEL // N_MIXERS
HEAD_DIM = 64
N_Q_HEADS = GROUP_W // HEAD_DIM
N_KV_HEADS = 2
Q_PER_KV = N_Q_HEADS // N_KV_HEADS
KV_W = N_KV_HEADS * HEAD_DIM
ATTN_SCALE = HEAD_DIM ** -0.5
Q_BLOCK = 128
WINDOW = 128
ROPE_THETA = 10000.0
ROPE_FREQS = HEAD_DIM // 4
CONV_WIDTH = 31
CONV_PAD = CONV_WIDTH // 2
HGRN_HEADS = 4
HGRN_KEY = GROUP_W // HGRN_HEADS
HGRN_VAL = GROUP_W // HGRN_HEADS
SCAN_CHUNK = 64
N_EXPERTS = 64
TOP_K = 8
N_EXPERT_GROUPS = 8
TOPK_GROUPS = 4
EXPERT_HIDDEN = 256
SHARED_HIDDEN = 256
ROUTED_SCALE = 2.5
MOE_BLOCK = 128
EPS = 1e-6
F32 = jnp.float32
IN_SPLITS = (GROUP_W, KV_W, KV_W, GROUP_W, KV_W, KV_W, GROUP_W, GROUP_W,
             GROUP_W, GROUP_W, GROUP_W, GROUP_W, GROUP_W)
D_IN = sum(IN_SPLITS)

kernel_name = 'hybrid_parallel_heads_dit_block'


def rms_norm(x, g):
    xf = x.astype(F32)
    y = xf * lax.rsqrt(jnp.mean(xf * xf, axis=-1, keepdims=True) + EPS)
    return (y * g.astype(F32)).astype(x.dtype)


def layer_norm(x, g, b):
    xf = x.astype(F32)
    xc = xf - jnp.mean(xf, axis=-1, keepdims=True)
    y = xc * lax.rsqrt(jnp.mean(xc * xc, axis=-1, keepdims=True) + EPS)
    return (y * g.astype(F32) + b.astype(F32)).astype(x.dtype)


def modulate(h, shift, scale):
    return h * (1.0 + scale) + shift


def split_cols(p):
    cuts = [int(v) for v in np.cumsum(IN_SPLITS)[:-1]]
    return jnp.split(p, cuts, axis=-1)


def split_heads(t, n_heads, dim):
    return t.reshape(t.shape[:-1] + (n_heads, dim))


def axial_rope(n):
    n_rows = n // GRID_W
    row = jnp.broadcast_to(jnp.arange(n_rows)[:, None], (n_rows, GRID_W)).reshape(n)
    col = jnp.broadcast_to(jnp.arange(GRID_W)[None, :], (n_rows, GRID_W)).reshape(n)
    inv_freq = jnp.power(ROPE_THETA, -jnp.arange(ROPE_FREQS, dtype=F32) / ROPE_FREQS)
    ang = jnp.stack([row.astype(F32)[:, None] * inv_freq,
                     col.astype(F32)[:, None] * inv_freq], axis=1)
    return jnp.cos(ang), jnp.sin(ang)


def apply_rope(x, rope):
    cos, sin = rope
    b, n, h, _ = x.shape
    xr = x.astype(F32).reshape(b, n, h, 2, 2, ROPE_FREQS)
    x1, x2 = xr[..., 0, :], xr[..., 1, :]
    cs, sn = cos[None, :, None], sin[None, :, None]
    out = jnp.stack([x1 * cs - x2 * sn, x1 * sn + x2 * cs], axis=-2)
    return out.reshape(b, n, h, HEAD_DIM).astype(x.dtype)


def attn_heads(qx, kx, vx, q_gain, k_gain, rope=None):
    q = rms_norm(split_heads(qx, N_Q_HEADS, HEAD_DIM), q_gain)
    k = rms_norm(split_heads(kx, N_KV_HEADS, HEAD_DIM), k_gain)
    if rope is not None:
        q, k = apply_rope(q, rope), apply_rope(k, rope)
    return q, k, split_heads(vx, N_KV_HEADS, HEAD_DIM)


def ctx_attention(q_c, k_c, v_c, sink=None):
    b, l = q_c.shape[:2]
    qg = q_c.reshape(b, l, N_KV_HEADS, Q_PER_KV, HEAD_DIM)
    s = jnp.einsum('bqkgd,bskd->bkgqs', qg, k_c, preferred_element_type=F32) * ATTN_SCALE
    if sink is not None:
        s_sink = jnp.broadcast_to(sink.astype(F32).reshape(1, N_KV_HEADS, Q_PER_KV, 1, 1), s.shape[:-1] + (1,))
        s = jnp.concatenate([s, s_sink], axis=-1)
    p = jax.nn.softmax(s, axis=-1)[..., :l].astype(v_c.dtype)
    return jnp.einsum('bkgqs,bskd->bqkgd', p, v_c).reshape(b, l, GROUP_W)


def global_attention(q, k, v, k_c, v_c):
    b, n = q.shape[:2]
    nb = n // Q_BLOCK
    k_all = jnp.concatenate([k, k_c], axis=1)
    v_all = jnp.concatenate([v, v_c], axis=1)
    q_blocks = q.reshape(b, nb, Q_BLOCK, N_KV_HEADS, Q_PER_KV, HEAD_DIM).transpose(1, 0, 2, 3, 4, 5)

    def block(qb):
        s = jnp.einsum('bqkgd,bskd->bkgqs', qb, k_all, preferred_element_type=F32) * ATTN_SCALE
        p = jax.nn.softmax(s, axis=-1).astype(v_all.dtype)
        return jnp.einsum('bkgqs,bskd->bqkgd', p, v_all)

    o = lax.map(block, q_blocks)
    return o.transpose(1, 0, 2, 3, 4, 5).reshape(b, n, GROUP_W)


def window_attention(q, k, v, k_c, v_c, sink):
    b, n = q.shape[:2]
    l = k_c.shape[1]
    nb = n // Q_BLOCK
    span = 3 * Q_BLOCK

    def band(t):
        tp = jnp.pad(t, ((0, 0), (WINDOW, WINDOW), (0, 0), (0, 0)))
        tp = tp.reshape(b, nb + 2, Q_BLOCK, N_KV_HEADS, HEAD_DIM)
        return jnp.concatenate([tp[:, :-2], tp[:, 1:-1], tp[:, 2:]], axis=2)

    kb, vb = band(k), band(v)
    qb = q.reshape(b, nb, Q_BLOCK, N_KV_HEADS, Q_PER_KV, HEAD_DIM)
    s_loc = jnp.einsum('bnqkgd,bnskd->bnkgqs', qb, kb, preferred_element_type=F32) * ATTN_SCALE
    q_pos = jnp.arange(nb)[:, None] * Q_BLOCK + jnp.arange(Q_BLOCK)[None, :]
    k_pos = jnp.arange(nb)[:, None] * Q_BLOCK - WINDOW + jnp.arange(span)[None, :]
    inside = ((jnp.abs(q_pos[:, :, None] - k_pos[:, None, :]) <= WINDOW)
              & (k_pos[:, None, :] >= 0) & (k_pos[:, None, :] < n))
    s_loc = jnp.where(inside[None, :, None, None], s_loc, -jnp.inf)
    s_ctx = jnp.einsum('bnqkgd,bskd->bnkgqs', qb, k_c, preferred_element_type=F32) * ATTN_SCALE
    s_sink = jnp.broadcast_to(sink.astype(F32).reshape(1, 1, N_KV_HEADS, Q_PER_KV, 1, 1), s_loc.shape[:-1] + (1,))
    p = jax.nn.softmax(jnp.concatenate([s_loc, s_ctx, s_sink], axis=-1), axis=-1).astype(v.dtype)
    o = (jnp.einsum('bnkgqs,bnskd->bnqkgd', p[..., :span], vb)
         + jnp.einsum('bnkgqs,bskd->bnqkgd', p[..., span:span + l], v_c))
    return o.reshape(b, n, GROUP_W)


def conformer_conv(val, gate, dw_w, dw_b, ln_g, ln_b, pw_w, pw_b):
    h = val * jax.nn.sigmoid(gate)
    h = lax.conv_general_dilated(h, dw_w[:, None, :].astype(h.dtype), window_strides=(1,),
                                 padding=[(CONV_PAD, CONV_PAD)],
                                 dimension_numbers=('NWC', 'WIO', 'NWC'),
                                 feature_group_count=GROUP_W) + dw_b
    h = jax.nn.silu(layer_norm(h, ln_g, ln_b))
    return h @ pw_w + pw_b


def gla_chunk_scan(q, k, v, logf, s0):
    b, n, h, _ = q.shape
    dv = v.shape[-1]
    nc = n // SCAN_CHUNK

    def chunks(t):
        return t.astype(F32).reshape(b, nc, SCAN_CHUNK, h, t.shape[-1]).transpose(1, 0, 3, 2, 4)

    lower = jnp.tril(jnp.ones((SCAN_CHUNK, SCAN_CHUNK), dtype=bool))[:, :, None]

    def step(state, inp):
        qc, kc, vc, gc = inp
        cum = jnp.cumsum(gc, axis=2)
        last = cum[:, :, -1:, :]
        rel = cum[:, :, :, None, :] - cum[:, :, None, :, :]
        decay = jnp.exp(jnp.where(lower, rel, -jnp.inf))
        scores = jnp.einsum('bhtd,bhsd,bhtsd->bhts', qc, kc, decay)
        out = (jnp.einsum('bhtd,bhde->bhte', qc * jnp.exp(cum), state)
               + jnp.einsum('bhts,bhse->bhte', scores, vc))
        new_state = (jnp.exp(last[:, :, 0, :])[..., None] * state
                     + jnp.einsum('bhsd,bhse->bhde', kc * jnp.exp(last - cum), vc))
        return new_state, out

    final, out = lax.scan(step, s0, (chunks(q), chunks(k), chunks(v), chunks(logf)))
    return out.transpose(1, 0, 3, 2, 4).reshape(b, n, h, dv), final


def hgrn_gates(f_logits, lb):
    z = f_logits.astype(F32)
    logf = jnp.logaddexp(jax.nn.log_sigmoid(z), jnp.log(lb) + jax.nn.log_sigmoid(-z))
    return (1.0 - lb) * jax.nn.sigmoid(-z), logf


def hgrn_out(o, g, onorm_g):
    y = rms_norm(o, onorm_g) * jax.nn.silu(split_heads(g, HGRN_HEADS, HGRN_VAL).astype(F32))
    return y.reshape(g.shape).astype(g.dtype)


def hgrn2_mixer(qx, ix, ffx, fbx, gx, qc, ic, ffc, fbc, gc, lb_fwd, lb_bwd, onorm_g, with_ctx):
    b = qx.shape[0]
    s0 = jnp.zeros((b, HGRN_HEADS, HGRN_KEY, HGRN_VAL), F32)
    o_lat, o_ctx = 0.0, 0.0
    for f_lat, f_ctx, lb, reverse in ((ffx, ffc, lb_fwd, False), (fbx, fbc, lb_bwd, True)):
        def orient(t):
            return jnp.flip(t, axis=1) if reverse else t
        k_c, logf_c = hgrn_gates(split_heads(orient(f_ctx), HGRN_HEADS, HGRN_KEY), lb)
        out_c, state = gla_chunk_scan(split_heads(orient(qc), HGRN_HEADS, HGRN_KEY), k_c,
                                      split_heads(orient(ic), HGRN_HEADS, HGRN_VAL), logf_c, s0)
        k_x, logf_x = hgrn_gates(split_heads(orient(f_lat), HGRN_HEADS, HGRN_KEY), lb)
        out_x, _ = gla_chunk_scan(split_heads(orient(qx), HGRN_HEADS, HGRN_KEY), k_x,
                                  split_heads(orient(ix), HGRN_HEADS, HGRN_VAL), logf_x, state)
        o_lat = o_lat + orient(out_x)
        if with_ctx:
            o_ctx = o_ctx + orient(out_c)
    y_lat = hgrn_out(o_lat, gx, onorm_g)
    y_ctx = hgrn_out(o_ctx, gc, onorm_g) if with_ctx else None
    return y_lat, y_ctx


def routed_experts(h, idx, w, w_gate, w_up, w_down):
    t, d = h.shape
    a = t * TOP_K
    flat_e = idx.reshape(a)
    order = jnp.argsort(flat_e)
    sorted_e = flat_e[order]
    counts = jnp.bincount(flat_e, length=N_EXPERTS)
    starts = jnp.cumsum(counts) - counts
    padded = (counts + MOE_BLOCK - 1) // MOE_BLOCK * MOE_BLOCK
    pad_ends = jnp.cumsum(padded)
    pad_starts = pad_ends - padded
    dest = pad_starts[sorted_e] + jnp.arange(a) - starts[sorted_e]
    n_blocks = -(-(a + N_EXPERTS * (MOE_BLOCK - 1)) // MOE_BLOCK)
    n_slots = n_blocks * MOE_BLOCK
    slot_src = jnp.full((n_slots,), a, jnp.int32).at[dest].set(order.astype(jnp.int32))
    valid = slot_src < a
    src = jnp.minimum(slot_src, a - 1)
    slot_tok = jnp.where(valid, src // TOP_K, 0)
    slot_w = jnp.where(valid, w.reshape(a)[src], 0.0).astype(h.dtype)
    block_e = jnp.minimum(jnp.searchsorted(pad_ends, jnp.arange(n_blocks) * MOE_BLOCK, side='right'), N_EXPERTS - 1)

    def run_block(args):
        tok, wt, e = args
        xb = h[tok]
        hid = jax.nn.silu(xb @ w_gate[e]) * (xb @ w_up[e])
        return (hid @ w_down[e]) * wt[:, None]

    y = lax.map(run_block, (slot_tok.reshape(n_blocks, MOE_BLOCK), slot_w.reshape(n_blocks, MOE_BLOCK), block_e))
    return jax.ops.segment_sum(y.reshape(n_slots, d), slot_tok, num_segments=t)


def moe_ffn(h, router_w, router_bias, w_gate, w_up, w_down, sh_gate, sh_up, sh_down):
    t = h.shape[0]
    scores = jax.nn.sigmoid(jnp.matmul(h, router_w, preferred_element_type=F32))
    sel = scores + router_bias.astype(F32)
    grouped = sel.reshape(t, N_EXPERT_GROUPS, N_EXPERTS // N_EXPERT_GROUPS)
    group_score = lax.top_k(grouped, 2)[0].sum(-1)
    _, top_groups = lax.top_k(group_score, TOPK_GROUPS)
    group_keep = (top_groups[..., None] == jnp.arange(N_EXPERT_GROUPS)).any(axis=1)
    expert_keep = jnp.repeat(group_keep, N_EXPERTS // N_EXPERT_GROUPS, axis=-1)
    _, idx = lax.top_k(jnp.where(expert_keep, sel, -jnp.inf), TOP_K)
    w = jnp.take_along_axis(scores, idx, axis=-1)
    w = w / jnp.sum(w, axis=-1, keepdims=True) * ROUTED_SCALE
    shared = (jax.nn.silu(h @ sh_gate) * (h @ sh_up)) @ sh_down
    return routed_experts(h, idx, w, w_gate, w_up, w_down) + shared


def hybrid_layer(x, cx, c, c_ctx, rope, w_mod, b_mod, norm1_g, norm2_g, w_in, w_out,
                 gqa_q_gain, gqa_k_gain, win_q_gain, win_k_gain, win_sink,
                 conv_dw_w, conv_dw_b, conv_ln_g, conv_ln_b, conv_pw_w, conv_pw_b,
                 lb_fwd, lb_bwd, hgrn_onorm_g, router_w, router_bias,
                 exp_w_gate, exp_w_up, exp_w_down, sh_w_gate, sh_w_up, sh_w_down, with_ctx):
    b, n, d = x.shape
    l = cx.shape[1]
    mx = (jax.nn.silu(c) @ w_mod + b_mod).reshape(b, 6, d).transpose(1, 0, 2)[:, :, None, :]
    mc = (jax.nn.silu(c_ctx) @ w_mod + b_mod).reshape(6, 1, 1, d)

    hx = modulate(rms_norm(x, norm1_g), mx[0], mx[1])
    hc = modulate(rms_norm(cx, norm1_g), mc[0], mc[1])
    (gq, gk, gv, wq, wk, wv, cv_val, cv_gate, rq, ri, rf_f, rf_b, rg) = split_cols(hx @ w_in)
    (gq_c, gk_c, gv_c, wq_c, wk_c, wv_c, cv_val_c, cv_gate_c,
     rq_c, ri_c, rf_f_c, rf_b_c, rg_c) = split_cols(hc @ w_in)

    q, k, v = attn_heads(gq, gk, gv, gqa_q_gain, gqa_k_gain, rope)
    gq_h, gk_h, gv_h = attn_heads(gq_c, gk_c, gv_c, gqa_q_gain, gqa_k_gain)
    y_glob = global_attention(q, k, v, gk_h, gv_h)

    y_conv = conformer_conv(cv_val, cv_gate, conv_dw_w, conv_dw_b, conv_ln_g, conv_ln_b, conv_pw_w, conv_pw_b)

    q, k, v = attn_heads(wq, wk, wv, win_q_gain, win_k_gain, rope)
    wq_h, wk_h, wv_h = attn_heads(wq_c, wk_c, wv_c, win_q_gain, win_k_gain)
    y_win = window_attention(q, k, v, wk_h, wv_h, win_sink)

    y_rec, yc_rec = hgrn2_mixer(rq, ri, rf_f, rf_b, rg, rq_c, ri_c, rf_f_c, rf_b_c, rg_c,
                                lb_fwd, lb_bwd, hgrn_onorm_g, with_ctx)

    x = x + mx[2] * (jnp.concatenate([y_glob, y_conv, y_win, y_rec], axis=-1) @ w_out)
    tokens = modulate(rms_norm(x, norm2_g), mx[3], mx[4]).reshape(b * n, d)
    if with_ctx:
        yc_glob = ctx_attention(gq_h, gk_h, gv_h)
        yc_conv = conformer_conv(cv_val_c, cv_gate_c, conv_dw_w, conv_dw_b, conv_ln_g, conv_ln_b, conv_pw_w, conv_pw_b)
        yc_win = ctx_attention(wq_h, wk_h, wv_h, win_sink)
        cx = cx + mc[2] * (jnp.concatenate([yc_glob, yc_conv, yc_win, yc_rec], axis=-1) @ w_out)
        hc2 = modulate(rms_norm(cx, norm2_g), mc[3], mc[4]).reshape(b * l, d)
        tokens = jnp.concatenate([tokens, hc2], axis=0)

    f = moe_ffn(tokens, router_w, router_bias, exp_w_gate, exp_w_up, exp_w_down, sh_w_gate, sh_w_up, sh_w_down)
    x = x + mx[5] * f[:b * n].reshape(b, n, d)
    if with_ctx:
        cx = cx + mc[5] * f[b * n:].reshape(b, l, d)
    return x, cx


def setup_inputs(seed: int = 0) -> dict:
    key = jax.random.key(seed)
    ks = jax.random.split(key, 31)
    d = D_MODEL

    def nrm(k, shape, s):
        return jax.random.normal(k, shape, F32) * s

    return {
        'x': nrm(ks[0], (BATCH, SEQ, d), 1.0),
        'c': nrm(ks[1], (BATCH, d), 1.0),
        'ctx': nrm(ks[2], (BATCH, CTX_LEN, d), 1.0),
        'c_ctx': nrm(ks[3], (d,), 1.0),
        'w_mod': nrm(ks[4], (DEPTH, d, 6 * d), 0.5 * d ** -0.5),
        'b_mod': nrm(ks[5], (DEPTH, 6 * d), 0.01),
        'norm1_g': 1.0 + nrm(ks[6], (DEPTH, d), 0.1),
        'norm2_g': 1.0 + nrm(ks[7], (DEPTH, d), 0.1),
        'w_in': nrm(ks[8], (DEPTH, d, D_IN), d ** -0.5),
        'w_out': nrm(ks[9], (DEPTH, N_MIXERS * GROUP_W, d), (N_MIXERS * GROUP_W) ** -0.5),
        'gqa_q_gain': 1.0 + nrm(ks[10], (DEPTH, HEAD_DIM), 0.1),
        'gqa_k_gain': 1.0 + nrm(ks[11], (DEPTH, HEAD_DIM), 0.1),
        'win_q_gain': 1.0 + nrm(ks[12], (DEPTH, HEAD_DIM), 0.1),
        'win_k_gain': 1.0 + nrm(ks[13], (DEPTH, HEAD_DIM), 0.1),
        'win_sink': nrm(ks[14], (DEPTH, N_Q_HEADS), 1.0),
        'conv_dw_w': nrm(ks[15], (DEPTH, CONV_WIDTH, GROUP_W), CONV_WIDTH ** -0.5),
        'conv_dw_b': nrm(ks[16], (DEPTH, GROUP_W), 0.01),
        'conv_ln_g': 1.0 + nrm(ks[17], (DEPTH, GROUP_W), 0.1),
        'conv_ln_b': nrm(ks[18], (DEPTH, GROUP_W), 0.01),
        'conv_pw_w': nrm(ks[19], (DEPTH, GROUP_W, GROUP_W), GROUP_W ** -0.5),
        'conv_pw_b': nrm(ks[20], (DEPTH, GROUP_W), 0.01),
        'hgrn_lb_logits': nrm(ks[21], (2, DEPTH, HGRN_HEADS * HGRN_KEY), 1.0),
        'hgrn_onorm_g': 1.0 + nrm(ks[22], (DEPTH, HGRN_VAL), 0.1),
        'router_w': nrm(ks[23], (DEPTH, d, N_EXPERTS), d ** -0.5),
        'router_bias': nrm(ks[24], (DEPTH, N_EXPERTS), 0.01),
        'exp_w_gate': nrm(ks[25], (DEPTH, N_EXPERTS, d, EXPERT_HIDDEN), d ** -0.5),
        'exp_w_up': nrm(ks[26], (DEPTH, N_EXPERTS, d, EXPERT_HIDDEN), d ** -0.5),
        'exp_w_down': nrm(ks[27], (DEPTH, N_EXPERTS, EXPERT_HIDDEN, d), EXPERT_HIDDEN ** -0.5),
        'sh_w_gate': nrm(ks[28], (DEPTH, d, SHARED_HIDDEN), d ** -0.5),
        'sh_w_up': nrm(ks[29], (DEPTH, d, SHARED_HIDDEN), d ** -0.5),
        'sh_w_down': nrm(ks[30], (DEPTH, SHARED_HIDDEN, d), SHARED_HIDDEN ** -0.5),
    }


def reference(x, c, ctx, c_ctx, w_mod, b_mod, norm1_g, norm2_g, w_in, w_out,
              gqa_q_gain, gqa_k_gain, win_q_gain, win_k_gain, win_sink,
              conv_dw_w, conv_dw_b, conv_ln_g, conv_ln_b, conv_pw_w, conv_pw_b,
              hgrn_lb_logits, hgrn_onorm_g, router_w, router_bias,
              exp_w_gate, exp_w_up, exp_w_down, sh_w_gate, sh_w_up, sh_w_down):
    rope = axial_rope(x.shape[1])
    lb = jnp.cumsum(jax.nn.softmax(hgrn_lb_logits.astype(F32), axis=1), axis=1)
    lb = (lb - lb[:, :1]).reshape(2, DEPTH, HGRN_HEADS, HGRN_KEY)
    cx = ctx
    for layer in range(DEPTH):
        x, cx = hybrid_layer(
            x, cx, c, c_ctx, rope, w_mod[layer], b_mod[layer], norm1_g[layer], norm2_g[layer],
            w_in[layer], w_out[layer], gqa_q_gain[layer], gqa_k_gain[layer],
            win_q_gain[layer], win_k_gain[layer], win_sink[layer],
            conv_dw_w[layer], conv_dw_b[layer], conv_ln_g[layer], conv_ln_b[layer],
            conv_pw_w[layer], conv_pw_b[layer], lb[0, layer], lb[1, layer], hgrn_onorm_g[layer],
            router_w[layer], router_bias[layer], exp_w_gate[layer], exp_w_up[layer], exp_w_down[layer],
            sh_w_gate[layer], sh_w_up[layer], sh_w_down[layer], layer < DEPTH - 1)
    return x
```

```python
import functools

import numpy as np
import jax
import jax.numpy as jnp
from jax import lax
from jax.experimental import pallas as pl
from jax.experimental.pallas import tpu as pltpu

F32 = jnp.float32
BF16 = jnp.bfloat16

GRID_W = 64
N_MIXERS = 4
GROUP_W = 256
HEAD_DIM = 64
N_Q_HEADS = 4
N_KV_HEADS = 2
KV_W = N_KV_HEADS * HEAD_DIM
ATTN_SCALE = HEAD_DIM ** -0.5
WINDOW = 128
ROPE_THETA = 10000.0
ROPE_FREQS = HEAD_DIM // 4
CONV_WIDTH = 31
CONV_PAD = CONV_WIDTH // 2
N_EXPERTS = 64
TOP_K = 8
N_EXPERT_GROUPS = 8
GROUP_SIZE = N_EXPERTS // N_EXPERT_GROUPS
TOPK_GROUPS = 4
ROUTED_SCALE = 2.5
EPS = 1e-6
NEG_INF = float("-inf")

VMEM_LIMIT_BYTES = 56 * 1024 * 1024
ROW_TILE = 256
ATTN_TQ = 256
ATTN_TK = 768
WIN_TQ = 256
SCAN_C = 128
CONV_HALO = 16
EXPERT_TM = 256
ROUTER_TT = 256

HEAD_PERM = (0, 2, 1, 3)


def _cparams(sem):
    return pltpu.CompilerParams(dimension_semantics=sem, vmem_limit_bytes=VMEM_LIMIT_BYTES)


def _softplus(z):
    return jnp.maximum(z, 0.0) + jnp.log1p(jnp.exp(-jnp.abs(z)))


def _sigmoid(z):
    return 1.0 / (1.0 + jnp.exp(-z))


def _silu(z):
    return z * _sigmoid(z)


def _rms_heads(t, bd, gain):
    ms = jnp.dot((t * t).astype(BF16), bd, preferred_element_type=F32)
    return t * lax.rsqrt(ms + EPS) * gain


def _rope(t, cos, sin_signed):
    w = t.shape[-1]
    lane = lax.broadcasted_iota(jnp.int32, t.shape, 1)
    even16 = (lane // ROPE_FREQS) % 2 == 0
    partner = jnp.where(even16, pltpu.roll(t, w - ROPE_FREQS, 1), pltpu.roll(t, ROPE_FREQS, 1))
    return t * cos + partner * sin_signed


def _mod_kernel(c_ref, w_ref, b_ref, o_ref):
    c = c_ref[...]
    o_ref[...] = jnp.dot(_silu(c), w_ref[...], preferred_element_type=F32,
                         precision=lax.Precision.HIGHEST) + b_ref[...]


def _modulation(c_rows, w_mod, b_mod, layer):
    r, d = c_rows.shape
    n = w_mod.shape[-1]
    tn = d
    return pl.pallas_call(
        _mod_kernel,
        grid=(n // tn,),
        in_specs=[pl.BlockSpec((r, d), lambda j: (0, 0)),
                  pl.BlockSpec((None, d, tn), lambda j: (layer, 0, j)),
                  pl.BlockSpec((None, 1, tn), lambda j: (layer, 0, j))],
        out_specs=pl.BlockSpec((r, tn), lambda j: (0, j)),
        out_shape=jax.ShapeDtypeStruct((r, n), F32),
        compiler_params=_cparams(("arbitrary",)),
        name="modulation",
    )(c_rows, w_mod, b_mod.reshape(b_mod.shape[0], 1, n))


def _in_proj_kernel(x_ref, mod_ref, g_ref, w_ref, cos_ref, sin_ref, gains_ref, bd_ref,
                    qg_ref, kg_ref, vg_ref, qw_ref, kw_ref, vw_ref, glu_ref, r_ref):
    x = x_ref[...]
    ms = jnp.mean(x * x, axis=-1, keepdims=True)
    h = x * lax.rsqrt(ms + EPS) * g_ref[...]
    h = h * (1.0 + mod_ref[1:2, :]) + mod_ref[0:1, :]
    p = jnp.dot(h.astype(BF16), w_ref[...], preferred_element_type=F32)

    cos = cos_ref[...]
    sin = sin_ref[...]
    bd = bd_ref[...]
    ck, sk, bdk = cos[:, :KV_W], sin[:, :KV_W], bd[:KV_W, :KV_W]

    def q_path(t, gain):
        return (_rope(_rms_heads(t, bd, gain), cos, sin) * ATTN_SCALE).astype(BF16)

    def k_path(t, gain):
        return _rope(_rms_heads(t, bdk, gain), ck, sk).astype(BF16)

    qg_ref[...] = q_path(p[:, 0:256], gains_ref[0:1, :])
    kg_ref[...] = k_path(p[:, 256:384], gains_ref[1:2, :KV_W])
    vg_ref[...] = p[:, 384:512].astype(BF16)
    qw_ref[...] = q_path(p[:, 512:768], gains_ref[2:3, :])
    kw_ref[...] = k_path(p[:, 768:896], gains_ref[3:4, :KV_W])
    vw_ref[...] = p[:, 896:1024].astype(BF16)
    glu_ref[...] = p[:, 1024:1280] * _sigmoid(p[:, 1280:1536])
    r_ref[...] = p[:, 1536:]


def _in_proj(xs, mod, g, w_bf16, cos, sin, gains, bd, ctx_tiles):
    b, s, d = xs.shape
    tm = ROW_TILE
    n_r = w_bf16.shape[1] - 1536

    def row(b_, i):
        return (b_, i, 0)

    out_shapes = [
        jax.ShapeDtypeStruct((b, s, GROUP_W), BF16), jax.ShapeDtypeStruct((b, s, KV_W), BF16),
        jax.ShapeDtypeStruct((b, s, KV_W), BF16), jax.ShapeDtypeStruct((b, s, GROUP_W), BF16),
        jax.ShapeDtypeStruct((b, s, KV_W), BF16), jax.ShapeDtypeStruct((b, s, KV_W), BF16),
        jax.ShapeDtypeStruct((b, s, GROUP_W), F32), jax.ShapeDtypeStruct((b, s, n_r), F32)]
    out_specs = [pl.BlockSpec((None, tm, sh.shape[-1]), row) for sh in out_shapes]
    return pl.pallas_call(
        _in_proj_kernel,
        grid=(b, s // tm),
        in_specs=[pl.BlockSpec((None, tm, d), row),
                  pl.BlockSpec((None, None, 6, d), lambda b_, i: (b_, jnp.where(i >= ctx_tiles, 1, 0), 0, 0)),
                  pl.BlockSpec((1, d), lambda b_, i: (0, 0)),
                  pl.BlockSpec(w_bf16.shape, lambda b_, i: (0, 0)),
                  pl.BlockSpec((tm, GROUP_W), lambda b_, i: (i, 0)),
                  pl.BlockSpec((tm, GROUP_W), lambda b_, i: (i, 0)),
                  pl.BlockSpec((4, GROUP_W), lambda b_, i: (0, 0)),
                  pl.BlockSpec((GROUP_W, GROUP_W), lambda b_, i: (0, 0))],
        out_specs=out_specs,
        out_shape=out_shapes,
        compiler_params=_cparams(("parallel", "parallel")),
        name="in_proj",
    )(xs, mod, g, w_bf16, cos, sin, gains, bd)


def _stack_heads(q):
    lane = lax.broadcasted_iota(jnp.int32, (q.shape[0], KV_W), 1)
    low = lane < HEAD_DIM
    a, bq = q[:, :KV_W], q[:, KV_W:]
    zero = jnp.zeros_like(a)
    return jnp.concatenate([jnp.where(low, a, zero), jnp.where(low, zero, a),
                            jnp.where(low, bq, zero), jnp.where(low, zero, bq)], axis=0)


def _unstack_heads(o, tq):
    lane = lax.broadcasted_iota(jnp.int32, (tq, KV_W), 1)
    low = lane < HEAD_DIM
    return jnp.concatenate([jnp.where(low, o[0:tq], o[tq:2 * tq]),
                            jnp.where(low, o[2 * tq:3 * tq], o[3 * tq:4 * tq])], axis=1)


def _qk(qz, k):
    return lax.dot_general(qz, k, (((1,), (1,)), ((), ())), preferred_element_type=F32)


def _flash_kernel(q_ref, k_ref, v_ref, o_ref, qz_sc, m_sc, l_sc, acc_sc, *, tq):
    j = pl.program_id(2)

    @pl.when(j == 0)
    def _():
        qz_sc[...] = _stack_heads(q_ref[...])
        m_sc[...] = jnp.full(m_sc.shape, NEG_INF, F32)
        l_sc[...] = jnp.zeros(l_sc.shape, F32)
        acc_sc[...] = jnp.zeros(acc_sc.shape, F32)

    s = _qk(qz_sc[...], k_ref[...])
    m_prev = m_sc[...]
    m_new = jnp.maximum(m_prev, jnp.max(s, axis=-1, keepdims=True))
    alpha = jnp.exp(m_prev - m_new)
    p = jnp.exp(s - m_new)
    l_sc[...] = alpha * l_sc[...] + jnp.sum(p, axis=-1, keepdims=True)
    acc_sc[...] = alpha * acc_sc[...] + jnp.dot(p.astype(BF16), v_ref[...], preferred_element_type=F32)
    m_sc[...] = m_new

    @pl.when(j == pl.num_programs(2) - 1)
    def _():
        o_ref[...] = _unstack_heads(acc_sc[...] / l_sc[...], tq).astype(o_ref.dtype)


def _global_attention(q, k, v, ctx_len):
    b, s, _ = q.shape
    tq, tk = ATTN_TQ, ATTN_TK
    q_off = ctx_len // tq
    n_q = (s - ctx_len) // tq
    return pl.pallas_call(
        functools.partial(_flash_kernel, tq=tq),
        grid=(b, n_q, s // tk),
        in_specs=[pl.BlockSpec((None, tq, GROUP_W), lambda b_, i, j: (b_, i + q_off, 0)),
                  pl.BlockSpec((None, tk, KV_W), lambda b_, i, j: (b_, j, 0)),
                  pl.BlockSpec((None, tk, KV_W), lambda b_, i, j: (b_, j, 0))],
        out_specs=pl.BlockSpec((None, tq, GROUP_W), lambda b_, i, j: (b_, i + q_off, 0)),
        out_shape=jax.ShapeDtypeStruct((b, s, GROUP_W), BF16),
        scratch_shapes=[pltpu.VMEM((4 * tq, KV_W), BF16), pltpu.VMEM((4 * tq, 1), F32),
                        pltpu.VMEM((4 * tq, 1), F32), pltpu.VMEM((4 * tq, KV_W), F32)],
        compiler_params=_cparams(("parallel", "parallel", "arbitrary")),
        name="global_attention",
    )(q, k, v)


def _sink_column(sink_ref, tq):
    return jnp.concatenate([jnp.full((tq, 1), sink_ref[r], F32) for r in range(N_Q_HEADS)], axis=0)


def _softmax_pv(parts, sink_col, tq):
    m = parts[0][0].max(axis=-1, keepdims=True)
    for s, _ in parts[1:]:
        m = jnp.maximum(m, s.max(axis=-1, keepdims=True))
    if sink_col is not None:
        m = jnp.maximum(m, sink_col)
        l = jnp.exp(sink_col - m)
    else:
        l = jnp.zeros_like(m)
    acc = None
    for s, v in parts:
        p = jnp.exp(s - m)
        l = l + p.sum(axis=-1, keepdims=True)
        pv = jnp.dot(p.astype(BF16), v, preferred_element_type=F32)
        acc = pv if acc is None else acc + pv
    return _unstack_heads(acc / l, tq)


def _window_kernel(sink_ref, q_ref, k_ref, v_ref, o_ref, *, tq, ctx_len):
    i = pl.program_id(1)
    s_total = k_ref.shape[0]
    kw = tq + 2 * WINDOW
    q_start = ctx_len + i * tq
    start = pl.multiple_of(jnp.clip(q_start - WINDOW, ctx_len, s_total - kw), WINDOW)
    qz = _stack_heads(q_ref[...])
    s_loc = _qk(qz, k_ref[pl.ds(start, kw), :])
    q_pos = q_start + lax.broadcasted_iota(jnp.int32, (4 * tq, kw), 0) % tq
    k_pos = start + lax.broadcasted_iota(jnp.int32, (4 * tq, kw), 1)
    s_loc = jnp.where(jnp.abs(q_pos - k_pos) <= WINDOW, s_loc, NEG_INF)
    s_ctx = _qk(qz, k_ref[0:ctx_len, :])
    out = _softmax_pv([(s_loc, v_ref[pl.ds(start, kw), :]), (s_ctx, v_ref[0:ctx_len, :])],
                      _sink_column(sink_ref, tq), tq)
    o_ref[...] = out.astype(o_ref.dtype)


def _window_attention(q, k, v, sink, ctx_len):
    b, s, _ = q.shape
    tq = WIN_TQ
    q_off = ctx_len // tq
    return pl.pallas_call(
        functools.partial(_window_kernel, tq=tq, ctx_len=ctx_len),
        grid_spec=pltpu.PrefetchScalarGridSpec(
            num_scalar_prefetch=1,
            grid=(b, (s - ctx_len) // tq),
            in_specs=[pl.BlockSpec((None, tq, GROUP_W), lambda b_, i, sk: (b_, i + q_off, 0)),
                      pl.BlockSpec((None, s, KV_W), lambda b_, i, sk: (b_, 0, 0)),
                      pl.BlockSpec((None, s, KV_W), lambda b_, i, sk: (b_, 0, 0))],
            out_specs=pl.BlockSpec((None, tq, GROUP_W), lambda b_, i, sk: (b_, i + q_off, 0))),
        out_shape=jax.ShapeDtypeStruct((b, s, GROUP_W), BF16),
        compiler_params=_cparams(("parallel", "parallel")),
        name="window_attention",
    )(sink, q, k, v)


def _ctx_attn_kernel(sink_ref, q_ref, k_ref, v_ref, y_hbm_ref, o_ref, *, use_sink):
    del y_hbm_ref
    tq = q_ref.shape[0]
    qz = _stack_heads(q_ref[...])
    s_ctx = _qk(qz, k_ref[...])
    sink_col = _sink_column(sink_ref, tq) if use_sink else None
    o_ref[...] = _softmax_pv([(s_ctx, v_ref[...])], sink_col, tq).astype(o_ref.dtype)


def _ctx_attention(q, k, v, sink, y, ctx_len, use_sink):
    b, s, _ = q.shape
    blk = lambda w: pl.BlockSpec((None, ctx_len, w), lambda b_, sk: (b_, 0, 0))
    return pl.pallas_call(
        functools.partial(_ctx_attn_kernel, use_sink=use_sink),
        grid_spec=pltpu.PrefetchScalarGridSpec(
            num_scalar_prefetch=1,
            grid=(b,),
            in_specs=[blk(GROUP_W), blk(KV_W), blk(KV_W), pl.BlockSpec(memory_space=pl.ANY)],
            out_specs=blk(GROUP_W)),
        out_shape=jax.ShapeDtypeStruct(y.shape, y.dtype),
        input_output_aliases={4: 0},
        compiler_params=_cparams(("parallel",)),
        name="ctx_attention",
    )(sink, q, k, v, y)


def _conv_kernel(h_ref, hp_ref, hn_ref, dw_ref, vec_ref, pw_ref, o_ref, buf, *, ctx_tiles):
    i = pl.program_id(1)
    t = h_ref.shape[0]
    first = (i == 0) | (i == ctx_tiles)
    last = (i == ctx_tiles - 1) | (i == pl.num_programs(1) - 1)
    buf[0:CONV_HALO, :] = jnp.where(first, 0.0, hp_ref[...])
    buf[CONV_HALO:CONV_HALO + t, :] = h_ref[...]
    buf[CONV_HALO + t:, :] = jnp.where(last, 0.0, hn_ref[...])
    base = CONV_HALO - CONV_PAD
    acc = jnp.zeros((t, GROUP_W), F32)
    for j in range(CONV_WIDTH):
        acc = acc + dw_ref[j:j + 1, :] * buf[base + j:base + j + t, :]
    acc = acc + vec_ref[0:1, :]
    mu = jnp.mean(acc, axis=-1, keepdims=True)
    xc = acc - mu
    y = xc * lax.rsqrt(jnp.mean(xc * xc, axis=-1, keepdims=True) + EPS)
    y = _silu(y * vec_ref[1:2, :] + vec_ref[2:3, :])
    o_ref[...] = (jnp.dot(y.astype(BF16), pw_ref[...], preferred_element_type=F32)
                  + vec_ref[3:4, :]).astype(o_ref.dtype)


def _conformer_conv(glu, dw_w, vecs, pw_bf16, ctx_len):
    b, s, w = glu.shape
    t = ROW_TILE
    hb = t // CONV_HALO
    n_h = s // CONV_HALO
    return pl.pallas_call(
        functools.partial(_conv_kernel, ctx_tiles=ctx_len // t),
        grid=(b, s // t),
        in_specs=[pl.BlockSpec((None, t, w), lambda b_, i: (b_, i, 0)),
                  pl.BlockSpec((None, CONV_HALO, w), lambda b_, i: (b_, jnp.maximum(i * hb - 1, 0), 0)),
                  pl.BlockSpec((None, CONV_HALO, w), lambda b_, i: (b_, jnp.minimum((i + 1) * hb, n_h - 1), 0)),
                  pl.BlockSpec((CONV_WIDTH, w), lambda b_, i: (0, 0)),
                  pl.BlockSpec((4, w), lambda b_, i: (0, 0)),
                  pl.BlockSpec((w, w), lambda b_, i: (0, 0))],
        out_specs=pl.BlockSpec((None, t, w), lambda b_, i: (b_, i, 0)),
        out_shape=jax.ShapeDtypeStruct((b, s, w), BF16),
        scratch_shapes=[pltpu.VMEM((t + 2 * CONV_HALO, w), F32)],
        compiler_params=_cparams(("parallel", "parallel")),
        name="conformer_conv",
    )(glu, glu, glu, dw_w, vecs, pw_bf16)


def _scan_level_table(c, rev):
    n_lev = int(np.log2(c))
    tau = np.arange(c)[::-1] if rev else np.arange(c)
    tt, ss = tau[:, None], tau[None, :]
    lev = np.full((c, c), -1, np.int32)
    lev[tt == ss] = n_lev
    for l in range(n_lev):
        m = 1 << l
        hit = (tt // (2 * m) == ss // (2 * m)) & (tt % (2 * m) >= m) & (ss % (2 * m) < m)
        lev[hit] = l
    return np.tile(lev, (1, N_Q_HEADS))


def _hgrn_kernel(*refs, c, rev, final):
    if final:
        (q_ref, v_ref, z_ref, lb_ref, lev_ref, hm_ref, bm_ref,
         oprev_ref, g_ref, ong_ref, bd_ref, o_ref, st_sc) = refs
    else:
        q_ref, v_ref, z_ref, lb_ref, lev_ref, hm_ref, bm_ref, o_ref, st_sc = refs

    @pl.when(pl.program_id(1) == 0)
    def _():
        st_sc[...] = jnp.zeros(st_sc.shape, F32)

    q = q_ref[...]
    v = v_ref[...]
    z = z_ref[...]
    lb = lb_ref[0:1, :]
    log_lb = lb_ref[1:2, :]
    ls_pos = -_softplus(-z)
    ls_neg = -_softplus(z)
    k = (1.0 - lb) * jnp.exp(ls_neg)
    b_arg = log_lb + ls_neg
    logf = jnp.maximum(ls_pos, b_arg) + jnp.log1p(jnp.exp(-jnp.abs(ls_pos - b_arg)))

    row = lax.broadcasted_iota(jnp.int32, (c, GROUP_W), 0)
    tau = (c - 1 - row) if rev else row

    def earlier(x_, d):
        return pltpu.roll(x_, (c - d) if rev else d, 0)

    def later(x_, d):
        return pltpu.roll(x_, d if rev else (c - d), 0)

    cum = logf
    d = 1
    while d < c:
        cum = cum + jnp.where(tau >= d, earlier(cum, d), 0.0)
        d *= 2
    last = cum[0:1, :] if rev else cum[c - 1:c, :]

    hm = hm_ref[...]

    def head_blocks(x_):
        return jnp.concatenate([x_ * hm[h:h + 1, :] for h in range(N_Q_HEADS)], axis=0).astype(BF16)

    lev = lev_ref[...]
    n_lev = int(np.log2(c))
    a = jnp.where(lev == n_lev, _qk(q.astype(BF16), head_blocks(k)), 0.0)
    qref = cum - logf
    kref = cum
    m = 1
    for l in range(n_lev):
        if m > 1:
            h2 = m // 2
            qref = jnp.where(tau % m < h2, qref, earlier(qref, h2))
            kref = jnp.where(tau % m >= h2, kref, later(kref, h2))
        q_l = (q * jnp.exp(cum - qref)).astype(BF16)
        k_l = head_blocks(k * jnp.exp(kref - cum))
        a = jnp.where(lev == l, _qk(q_l, k_l), a)
        m *= 2

    st = st_sc[...]
    out = (jnp.dot(a.astype(BF16), head_blocks(v), preferred_element_type=F32)
           + _qk((q * jnp.exp(cum)).astype(BF16), st.astype(BF16)))
    kd = (k * jnp.exp(last - cum)).astype(BF16)
    upd = lax.dot_general(v.astype(BF16), kd, (((0,), (0,)), ((), ())), preferred_element_type=F32)
    st_sc[...] = st * jnp.exp(last) + bm_ref[...] * upd

    if final:
        tot = out + oprev_ref[...]
        y = _rms_heads(tot, bd_ref[...], ong_ref[...]) * _silu(g_ref[...])
        o_ref[...] = y.astype(o_ref.dtype)
    else:
        o_ref[...] = out


def _hgrn_direction(r_all, lb2, consts, ctx_len, z_col, rev, final_inputs=None):
    b, s, _ = r_all.shape
    c = SCAN_C
    lc = ctx_len // c
    nc = s // c
    lev, hm, bm = consts

    if rev:
        def pos(ci):
            return jnp.where(ci < lc, lc - 1 - ci, lc + nc - 1 - ci)
    else:
        def pos(ci):
            return ci

    def col(cb):
        return pl.BlockSpec((None, c, GROUP_W), lambda b_, ci: (b_, pos(ci), cb))

    def const(a):
        return pl.BlockSpec(a.shape, lambda b_, ci: (0,) * a.ndim)

    in_specs = [col(0), col(1), col(z_col), const(lb2), const(lev), const(hm), const(bm)]
    args = [r_all, r_all, r_all, lb2, lev, hm, bm]
    final = final_inputs is not None
    if final:
        o_prev, ong, bd = final_inputs
        in_specs += [pl.BlockSpec((None, c, GROUP_W), lambda b_, ci: (b_, pos(ci), 0)), col(4), const(ong), const(bd)]
        args += [o_prev, r_all, ong, bd]
    return pl.pallas_call(
        functools.partial(_hgrn_kernel, c=c, rev=rev, final=final),
        grid=(b, nc),
        in_specs=in_specs,
        out_specs=pl.BlockSpec((None, c, GROUP_W), lambda b_, ci: (b_, pos(ci), 0)),
        out_shape=jax.ShapeDtypeStruct((b, s, GROUP_W), BF16 if final else F32),
        scratch_shapes=[pltpu.VMEM((GROUP_W, GROUP_W), F32)],
        compiler_params=_cparams(("parallel", "arbitrary")),
        name="hgrn_bwd" if rev else "hgrn_fwd",
    )(*args)


def _out_proj_kernel(yg_ref, yc_ref, yw_ref, yr_ref, w_ref, x_ref, mod_ref, g_ref, rw_ref,
                     xo_ref, tok_ref, lg_ref):
    acc = jnp.dot(yg_ref[...], w_ref[0:256, :], preferred_element_type=F32)
    acc += jnp.dot(yc_ref[...], w_ref[256:512, :], preferred_element_type=F32)
    acc += jnp.dot(yw_ref[...], w_ref[512:768, :], preferred_element_type=F32)
    acc += jnp.dot(yr_ref[...], w_ref[768:1024, :], preferred_element_type=F32)
    x = x_ref[...] + mod_ref[2:3, :] * acc
    xo_ref[...] = x
    ms = jnp.mean(x * x, axis=-1, keepdims=True)
    tok = x * lax.rsqrt(ms + EPS) * g_ref[...]
    tok = tok * (1.0 + mod_ref[4:5, :]) + mod_ref[3:4, :]
    tok_ref[...] = tok.astype(BF16)
    lg_ref[...] = lax.dot_general(rw_ref[...], tok, (((1,), (1,)), ((), ())),
                                  preferred_element_type=F32, precision=lax.Precision.HIGHEST)


def _out_proj(ys, w_bf16, xs, mod, g, rw_t, ctx_tiles):
    b, s, d = xs.shape
    tm = ROW_TILE
    row = lambda b_, i: (b_, i, 0)
    yspec = pl.BlockSpec((None, tm, GROUP_W), row)
    return pl.pallas_call(
        _out_proj_kernel,
        grid=(b, s // tm),
        in_specs=[yspec, yspec, yspec, yspec,
                  pl.BlockSpec(w_bf16.shape, lambda b_, i: (0, 0)),
                  pl.BlockSpec((None, tm, d), row),
                  pl.BlockSpec((None, None, 6, d), lambda b_, i: (b_, jnp.where(i >= ctx_tiles, 1, 0), 0, 0)),
                  pl.BlockSpec((1, d), lambda b_, i: (0, 0)),
                  pl.BlockSpec(rw_t.shape, lambda b_, i: (0, 0))],
        out_specs=[pl.BlockSpec((None, tm, d), row), pl.BlockSpec((None, tm, d), row),
                   pl.BlockSpec((None, N_EXPERTS, tm), lambda b_, i: (b_, 0, i))],
        out_shape=[jax.ShapeDtypeStruct((b, s, d), F32), jax.ShapeDtypeStruct((b, s, d), BF16),
                   jax.ShapeDtypeStruct((b, N_EXPERTS, s), F32)],
        compiler_params=_cparams(("parallel", "parallel")),
        name="out_proj",
    )(*ys, w_bf16, xs, mod, g, rw_t)


def _first_max(cur, n):
    mx = jnp.max(cur, axis=0, keepdims=True)
    iota = lax.broadcasted_iota(jnp.int32, cur.shape, 0).astype(F32)
    fi = jnp.min(jnp.where(cur == mx, iota, float(n)), axis=0, keepdims=True)
    return mx, fi, iota == fi


def _router_kernel(lg_ref, bias_ref, idx_ref, w_ref):
    scores = _sigmoid(lg_ref[...])
    sel = scores + bias_ref[...]
    gs = []
    for gi in range(N_EXPERT_GROUPS):
        g = sel[gi * GROUP_SIZE:(gi + 1) * GROUP_SIZE, :]
        m1, _, hit = _first_max(g, GROUP_SIZE)
        m2 = jnp.max(jnp.where(hit, NEG_INF, g), axis=0, keepdims=True)
        gs.append(m1 + m2)
    cur = jnp.concatenate(gs, axis=0)
    keep_f = jnp.zeros(cur.shape, F32)
    for _ in range(TOPK_GROUPS):
        _, _, hit = _first_max(cur, N_EXPERT_GROUPS)
        keep_f = jnp.where(hit, 1.0, keep_f)
        cur = jnp.where(hit, NEG_INF, cur)
    cur = jnp.concatenate(
        [jnp.where(keep_f[gi:gi + 1, :] > 0.5, sel[gi * GROUP_SIZE:(gi + 1) * GROUP_SIZE, :], NEG_INF)
         for gi in range(N_EXPERT_GROUPS)], axis=0)
    idxs, ws = [], []
    for _ in range(TOP_K):
        _, fi, hit = _first_max(cur, N_EXPERTS)
        idxs.append(fi)
        ws.append(jnp.sum(jnp.where(hit, scores, 0.0), axis=0, keepdims=True))
        cur = jnp.where(hit, NEG_INF, cur)
    w = jnp.concatenate(ws, axis=0)
    idx_ref[...] = jnp.concatenate(idxs, axis=0).astype(jnp.int32)
    w_ref[...] = w / jnp.sum(w, axis=0, keepdims=True) * ROUTED_SCALE


def _router(logits_t, bias_col):
    b, e, s = logits_t.shape
    tt = ROUTER_TT
    return pl.pallas_call(
        _router_kernel,
        grid=(b, s // tt),
        in_specs=[pl.BlockSpec((None, e, tt), lambda b_, i: (b_, 0, i)),
                  pl.BlockSpec((e, 1), lambda b_, i: (0, 0))],
        out_specs=[pl.BlockSpec((None, TOP_K, tt), lambda b_, i: (b_, 0, i)),
                   pl.BlockSpec((None, TOP_K, tt), lambda b_, i: (b_, 0, i))],
        out_shape=[jax.ShapeDtypeStruct((b, TOP_K, s), jnp.int32), jax.ShapeDtypeStruct((b, TOP_K, s), F32)],
        compiler_params=_cparams(("parallel", "parallel")),
        name="router",
    )(logits_t, bias_col)


def _expert_kernel(be_ref, nu_ref, x_ref, sw_ref, wg_ref, wu_ref, wd_ref, y_ref):
    i = pl.program_id(0)

    @pl.when(i < nu_ref[0])
    def _():
        x = x_ref[...]
        hg = jnp.dot(x, wg_ref[...].astype(BF16), preferred_element_type=F32)
        hu = jnp.dot(x, wu_ref[...].astype(BF16), preferred_element_type=F32)
        tm = x.shape[0]
        eye = lax.broadcasted_iota(jnp.int32, (tm, tm), 0) == lax.broadcasted_iota(jnp.int32, (tm, tm), 1)
        w_col = jnp.sum(jnp.where(eye, sw_ref[...], 0.0), axis=1, keepdims=True)
        hid = (_silu(hg) * hu * w_col).astype(BF16)
        y_ref[...] = jnp.dot(hid, wd_ref[...].astype(BF16), preferred_element_type=F32).astype(y_ref.dtype)

    @pl.when(i >= nu_ref[0])
    def _():
        y_ref[...] = jnp.zeros(y_ref.shape, y_ref.dtype)


def _routed_experts(xg, slot_w, block_e, n_used, w_gate, w_up, w_down, layer):
    n_slots, d = xg.shape
    tm = EXPERT_TM
    n_blocks = n_slots // tm
    hdim = w_gate.shape[-1]
    return pl.pallas_call(
        _expert_kernel,
        grid_spec=pltpu.PrefetchScalarGridSpec(
            num_scalar_prefetch=2,
            grid=(n_blocks,),
            in_specs=[pl.BlockSpec((tm, d), lambda i, be, nu: (i, 0)),
                      pl.BlockSpec((None, 1, tm), lambda i, be, nu: (i, 0, 0)),
                      pl.BlockSpec((None, None, d, hdim), lambda i, be, nu: (layer, be[i], 0, 0)),
                      pl.BlockSpec((None, None, d, hdim), lambda i, be, nu: (layer, be[i], 0, 0)),
                      pl.BlockSpec((None, None, hdim, d), lambda i, be, nu: (layer, be[i], 0, 0))],
            out_specs=pl.BlockSpec((tm, d), lambda i, be, nu: (i, 0))),
        out_shape=jax.ShapeDtypeStruct((n_slots, d), BF16),
        compiler_params=_cparams(("arbitrary",)),
        name="routed_experts",
    )(block_e, n_used, xg, slot_w.reshape(n_blocks, 1, tm), w_gate, w_up, w_down)


def _combine_kernel(yg_ref, tok_ref, sg_ref, su_ref, sd_ref, x_ref, mod_ref, o_ref):
    routed = yg_ref[0].astype(F32)
    for j in range(1, TOP_K):
        routed = routed + yg_ref[j].astype(F32)
    tok = tok_ref[...]
    hid = _silu(jnp.dot(tok, sg_ref[...], preferred_element_type=F32)) * jnp.dot(
        tok, su_ref[...], preferred_element_type=F32)
    shared = jnp.dot(hid.astype(BF16), sd_ref[...], preferred_element_type=F32)
    o_ref[...] = x_ref[...] + mod_ref[5:6, :] * (routed + shared)


def _combine(yg, tok, sg, su, sd, xs, mod, ctx_tiles):
    b, s, d = xs.shape
    tm = ROW_TILE
    tiles = s // tm
    row = lambda b_, i: (b_, i, 0)
    const = lambda a: pl.BlockSpec(a.shape, lambda b_, i: (0, 0))
    return pl.pallas_call(
        _combine_kernel,
        grid=(b, tiles),
        in_specs=[pl.BlockSpec((TOP_K, tm, d), lambda b_, i: (0, b_ * tiles + i, 0)),
                  pl.BlockSpec((None, tm, d), row), const(sg), const(su), const(sd),
                  pl.BlockSpec((None, tm, d), row),
                  pl.BlockSpec((None, None, 6, d), lambda b_, i: (b_, jnp.where(i >= ctx_tiles, 1, 0), 0, 0))],
        out_specs=pl.BlockSpec((None, tm, d), row),
        out_shape=jax.ShapeDtypeStruct((b, s, d), F32),
        compiler_params=_cparams(("parallel", "parallel")),
        name="moe_combine",
    )(yg, tok, sg, su, sd, xs, mod)


def _dispatch_tables(idx, w):
    t = idx.shape[0]
    a = t * TOP_K
    tm = EXPERT_TM
    n_blocks = -(-(a + N_EXPERTS * (tm - 1)) // tm)
    n_slots = n_blocks * tm
    flat_e = idx.reshape(a)
    order = jnp.argsort(flat_e)
    sorted_e = flat_e[order]
    counts = jnp.bincount(flat_e, length=N_EXPERTS)
    starts = jnp.cumsum(counts) - counts
    padded = (counts + tm - 1) // tm * tm
    pad_ends = jnp.cumsum(padded)
    pad_starts = pad_ends - padded
    dest = (pad_starts[sorted_e] + jnp.arange(a) - starts[sorted_e]).astype(jnp.int32)
    slot_tok = jnp.zeros((n_slots,), jnp.int32).at[dest].set((order // TOP_K).astype(jnp.int32))
    slot_w = jnp.zeros((n_slots,), F32).at[dest].set(w.reshape(a)[order])
    tok_slot = jnp.zeros((a,), jnp.int32).at[order].set(dest).reshape(t, TOP_K).T
    block_e = jnp.minimum(jnp.searchsorted(pad_ends, jnp.arange(n_blocks) * tm, side="right"),
                          N_EXPERTS - 1).astype(jnp.int32)
    n_used = (pad_ends[-1] // tm).astype(jnp.int32).reshape(1)
    return slot_tok, slot_w, block_e, n_used, tok_slot


def _rope_tables(n, ctx_len):
    n_rows = n // GRID_W
    row = np.repeat(np.arange(n_rows), GRID_W).astype(np.float32)
    colp = np.tile(np.arange(GRID_W), n_rows).astype(np.float32)
    inv_freq = jnp.power(ROPE_THETA, -jnp.arange(ROPE_FREQS, dtype=F32) / ROPE_FREQS)
    ang_r = jnp.asarray(row)[:, None] * inv_freq
    ang_c = jnp.asarray(colp)[:, None] * inv_freq
    cos = jnp.concatenate([jnp.cos(ang_r)] * 2 + [jnp.cos(ang_c)] * 2, axis=1)
    sin = jnp.concatenate([-jnp.sin(ang_r), jnp.sin(ang_r), -jnp.sin(ang_c), jnp.sin(ang_c)], axis=1)
    cos = jnp.concatenate([jnp.ones((ctx_len, HEAD_DIM), F32), cos], axis=0)
    sin = jnp.concatenate([jnp.zeros((ctx_len, HEAD_DIM), F32), sin], axis=0)
    return jnp.tile(cos, (1, N_Q_HEADS)), jnp.tile(sin, (1, N_Q_HEADS))


def _perm_cols():
    return np.concatenate([np.arange(h * HEAD_DIM, (h + 1) * HEAD_DIM) for h in HEAD_PERM])


def kernel(x, c, ctx, c_ctx, w_mod, b_mod, norm1_g, norm2_g, w_in, w_out, gqa_q_gain, gqa_k_gain,
           win_q_gain, win_k_gain, win_sink, conv_dw_w, conv_dw_b, conv_ln_g, conv_ln_b, conv_pw_w,
           conv_pw_b, hgrn_lb_logits, hgrn_onorm_g, router_w, router_bias, exp_w_gate, exp_w_up,
           exp_w_down, sh_w_gate, sh_w_up, sh_w_down):
    b, n, d = x.shape
    ctx_len = ctx.shape[1]
    depth = w_mod.shape[0]
    s = ctx_len + n
    ctx_tiles = ctx_len // ROW_TILE
    assert ctx_len % ROW_TILE == 0 and n % ROW_TILE == 0 and ctx_len % SCAN_C == 0
    assert s % ATTN_TK == 0 and n % ATTN_TQ == 0 and n % WIN_TQ == 0

    xs = jnp.concatenate([ctx, x], axis=1)
    cos, sin = _rope_tables(n, ctx_len)
    hd_idx = np.arange(GROUP_W) // HEAD_DIM
    bd = jnp.asarray((hd_idx[:, None] == hd_idx[None, :]).astype(np.float32) / HEAD_DIM, BF16)
    bm = jnp.asarray((hd_idx[:, None] == hd_idx[None, :]).astype(np.float32))
    hm = jnp.asarray((np.arange(N_Q_HEADS)[:, None] == hd_idx[None, :]).astype(np.float32))
    lev_f = jnp.asarray(_scan_level_table(SCAN_C, False))
    lev_b = jnp.asarray(_scan_level_table(SCAN_C, True))

    lb = jnp.cumsum(jax.nn.softmax(hgrn_lb_logits.astype(F32), axis=1), axis=1)
    lb = lb - lb[:, :1]

    perm = _perm_cols()
    in_cols = np.concatenate([perm, np.arange(256, 512), 512 + perm, np.arange(768, w_in.shape[-1])])
    out_rows = np.concatenate([perm, np.arange(256, 512), 512 + perm, np.arange(768, 1024)])
    sink_perm = np.asarray(HEAD_PERM)

    c_rows = jnp.concatenate([c, c_ctx[None, :], jnp.zeros((8 - b - 1, d), F32)], axis=0)

    for layer in range(depth):
        mod_rows = _modulation(c_rows, w_mod, b_mod, layer)
        mod_lat = mod_rows[:b].reshape(b, 6, d)
        mod_ctx = jnp.broadcast_to(mod_rows[b].reshape(1, 6, d), (b, 6, d))
        mod = jnp.stack([mod_ctx, mod_lat], axis=1)

        w_in_l = w_in[layer][:, in_cols].astype(BF16)
        w_out_l = w_out[layer][out_rows, :].astype(BF16)
        gains = jnp.stack([jnp.tile(gqa_q_gain[layer], 4), jnp.tile(gqa_k_gain[layer], 4),
                           jnp.tile(win_q_gain[layer], 4), jnp.tile(win_k_gain[layer], 4)], axis=0)

        qg, kg, vg, qw, kw, vw, glu, r_all = _in_proj(
            xs, mod, norm1_g[layer].reshape(1, d), w_in_l, cos, sin, gains, bd, ctx_tiles)

        sink = win_sink[layer][sink_perm]
        y_glob = _global_attention(qg, kg, vg, ctx_len)
        y_glob = _ctx_attention(qg, kg, vg, sink, y_glob, ctx_len, use_sink=False)
        y_win = _window_attention(qw, kw, vw, sink, ctx_len)
        y_win = _ctx_attention(qw, kw, vw, sink, y_win, ctx_len, use_sink=True)

        conv_vecs = jnp.stack([conv_dw_b[layer], conv_ln_g[layer], conv_ln_b[layer], conv_pw_b[layer]], axis=0)
        y_conv = _conformer_conv(glu, conv_dw_w[layer], conv_vecs, conv_pw_w[layer].astype(BF16), ctx_len)

        lb_f = jnp.stack([lb[0, layer], jnp.log(lb[0, layer])], axis=0)
        lb_b = jnp.stack([lb[1, layer], jnp.log(lb[1, layer])], axis=0)
        o_f = _hgrn_direction(r_all, lb_f, (lev_f, hm, bm), ctx_len, z_col=2, rev=False)
        ong = jnp.tile(hgrn_onorm_g[layer], 4).reshape(1, GROUP_W)
        y_rec = _hgrn_direction(r_all, lb_b, (lev_b, hm, bm), ctx_len, z_col=3, rev=True,
                                final_inputs=(o_f, ong, bd))

        xs, tok, logits_t = _out_proj((y_glob, y_conv, y_win, y_rec), w_out_l, xs, mod,
                                      norm2_g[layer].reshape(1, d), router_w[layer].T, ctx_tiles)

        idx_t, w_t = _router(logits_t, router_bias[layer].reshape(N_EXPERTS, 1))
        t = b * s
        idx = idx_t.transpose(0, 2, 1).reshape(t, TOP_K)
        wts = w_t.transpose(0, 2, 1).reshape(t, TOP_K)
        slot_tok, slot_w, block_e, n_used, tok_slot = _dispatch_tables(idx, wts)

        tok2 = tok.reshape(t, d)
        xg = jnp.take(tok2, slot_tok, axis=0)
        y = _routed_experts(xg, slot_w, block_e, n_used, exp_w_gate, exp_w_up, exp_w_down, layer)
        yg = jnp.take(y, tok_slot, axis=0)
        xs = _combine(yg, tok, sh_w_gate[layer].astype(BF16), sh_w_up[layer].astype(BF16),
                      sh_w_down[layer].astype(BF16), xs, mod, ctx_tiles)

    return xs[:, ctx_len:, :]
```
